```python
import functools
import jax, jax.numpy as jnp
from jax import lax
import numpy as np

D_MODEL = 2048
BATCH = 2
SEQ = 4096
DEPTH = 1
DEC_BATCH = 32
DEC_SEQ = 4
PAST_LEN = 16384
PAGE_SIZE = 128

RWKV_HEADS = 16
RWKV_HEAD_DIM = 64
RWKV_WIDTH = RWKV_HEADS * RWKV_HEAD_DIM
DECAY_LORA = 64
AAA_LORA = 64
GATE_LORA = 160
RWKV_PROJ = 3 * RWKV_WIDTH + DECAY_LORA + AAA_LORA + GATE_LORA
GN_EPS = 64e-5
ATTN_HEADS = 8
ATTN_HEAD_DIM = 128
ATTN_WIDTH = ATTN_HEADS * ATTN_HEAD_DIM
MOBA_BLOCK = 256
MOBA_TOPK = 3
Q_CHUNK = 32
IN_PROJ = RWKV_PROJ + 3 * ATTN_WIDTH + 2 * D_MODEL
N_GROUPS = 4
EXPERTS_PER_GROUP = 8
N_EXPERTS = N_GROUPS * EXPERTS_PER_GROUP
EXPERT_TOPK = 2
D_EXPERT = 256
PLE_DIM = 256
NORM_EPS = 1e-6

kernel_name = "rwkv7_moba_hier_moe_step"


def _rmsnorm(x, g):
    xf = x.astype(jnp.float32)
    y = xf * lax.rsqrt(jnp.mean(xf * xf, axis=-1, keepdims=True) + NORM_EPS)
    return (y * g.astype(jnp.float32)).astype(x.dtype)


def _alibi_slopes(n):
    return jnp.asarray(2.0 ** (-8.0 * np.arange(1, n + 1) / n), dtype=jnp.float32)


def _rwkv7(z, shift_prev, wkv_prev, lw):
    B, T, _ = z.shape
    H, N, W = RWKV_HEADS, RWKV_HEAD_DIM, RWKV_WIDTH
    f32 = jnp.float32
    prev = jnp.concatenate([shift_prev[:, None, :].astype(z.dtype), z[:, :-1]], axis=1)
    zs = z + (prev - z) * lw['shift_mu']
    r, k, v, hw, ha, hg = jnp.split(zs, [W, 2 * W, 3 * W, 3 * W + DECAY_LORA, 3 * W + DECAY_LORA + AAA_LORA], axis=-1)
    w_log = -jax.nn.softplus(-(lw['decay_w0'] + jnp.tanh(hw) @ lw['decay_w2']).astype(f32)) - 0.5
    decay = jnp.exp(-jnp.exp(w_log))
    a = jax.nn.sigmoid((lw['iclr_a0'] + ha @ lw['iclr_a2']).astype(f32))
    g = (jax.nn.sigmoid(hg) @ lw['gate_g2']).astype(f32)
    heads = lambda t: t.reshape(B, T, H, N)
    kk = heads((k * lw['k_k']).astype(f32))
    kk = kk * lax.rsqrt(jnp.maximum(jnp.sum(kk * kk, axis=-1, keepdims=True), 1e-24))
    k_mod = k.astype(f32) * (1.0 + (a - 1.0) * lw['k_a'].astype(f32))
    r_h, k_h, v_h, a_h, w_h = heads(r.astype(f32)), heads(k_mod), heads(v.astype(f32)), heads(a), heads(decay)
    b_h = kk * a_h

    def step(S, inp):
        r_t, w_t, k_t, v_t, kk_t, b_t = inp
        sa = jnp.einsum('bhij,bhj->bhi', S, -kk_t)
        S = S * w_t[:, :, None, :] + sa[..., None] * b_t[:, :, None, :] + v_t[..., None] * k_t[:, :, None, :]
        return S, jnp.einsum('bhij,bhj->bhi', S, r_t)

    seq_first = lambda t: jnp.swapaxes(t, 0, 1)
    S_fin, y = lax.scan(step, wkv_prev.astype(f32), tuple(seq_first(t) for t in (r_h, w_h, k_h, v_h, kk, b_h)))
    y = seq_first(y)
    y_mean = jnp.mean(y, axis=-1, keepdims=True)
    y_var = jnp.mean(jnp.square(y - y_mean), axis=-1, keepdims=True)
    y_n = ((y - y_mean) * lax.rsqrt(y_var + GN_EPS)).reshape(B, T, W) * lw['ln_x_w'].astype(f32) + lw['ln_x_b'].astype(f32)
    bonus = jnp.sum(r_h * k_h * lw['r_k'].astype(f32), axis=-1, keepdims=True) * v_h
    out = (y_n + bonus.reshape(B, T, W)) * g
    return out.astype(z.dtype), S_fin.astype(wkv_prev.dtype), z[:, -1]


def _moba_attend(q, t_pos, k_sel, v_sel, sel_pos, sel_valid, k_own, v_own, own_pos, slopes):
    f32 = jnp.float32
    scale = ATTN_HEAD_DIM ** -0.5
    s_sel = jnp.einsum('bhqd,bhqsjd->bhqsj', q, k_sel).astype(f32) * scale
    B, H, Q, S, BS = s_sel.shape
    dist_sel = (t_pos[None, None, :, None, None] - sel_pos).astype(f32)
    s_sel = jnp.where(sel_valid[..., None], s_sel - slopes[None, :, None, None, None] * dist_sel, -jnp.inf)
    s_own = jnp.einsum('bhqd,bhjd->bhqj', q, k_own).astype(f32) * scale
    dist_own = t_pos[:, None] - own_pos[None, :]
    s_own = jnp.where((dist_own >= 0)[None, None], s_own - slopes[None, :, None, None] * dist_own.astype(f32), -jnp.inf)
    p = jax.nn.softmax(jnp.concatenate([s_sel.reshape(B, H, Q, S * BS), s_own], axis=-1), axis=-1)
    p_sel = p[..., :S * BS].reshape(B, H, Q, S, BS).astype(v_own.dtype)
    p_own = p[..., S * BS:].astype(v_own.dtype)
    return jnp.einsum('bhqsj,bhqsjd->bhqd', p_sel, v_sel) + jnp.einsum('bhqj,bhjd->bhqd', p_own, v_own)


def _moba_prompt(q, k, v, slopes):
    B, T, H, D = q.shape
    f32 = jnp.float32
    nb = -(-T // MOBA_BLOCK)
    pad = ((0, 0), (0, 0), (0, nb * MOBA_BLOCK - T), (0, 0))
    qh = jnp.transpose(q, (0, 2, 1, 3))
    kb = jnp.pad(jnp.transpose(k, (0, 2, 1, 3)), pad).reshape(B, H, nb, MOBA_BLOCK, D)
    vb = jnp.pad(jnp.transpose(v, (0, 2, 1, 3)), pad).reshape(B, H, nb, MOBA_BLOCK, D)
    kmean = jnp.mean(kb, axis=3, dtype=f32)
    t_all = jnp.arange(T)
    q_blk = t_all // MOBA_BLOCK
    gate = jnp.einsum('bhtd,bhnd->bhtn', qh.astype(f32), kmean)
    gate = jnp.where(jnp.arange(nb)[None, :] < q_blk[:, None], gate, -jnp.inf)
    n_sel = min(MOBA_TOPK, nb)
    _, idx = lax.top_k(gate, n_sel)
    valid = idx < q_blk[:, None]
    bi = jnp.arange(B)[:, None, None, None]
    hi = jnp.arange(H)[None, :, None, None]

    def chunk(ci):
        t0 = ci * Q_CHUNK
        qc = lax.dynamic_slice_in_dim(qh, t0, Q_CHUNK, axis=2)
        ic = lax.dynamic_slice_in_dim(idx, t0, Q_CHUNK, axis=2)
        vc = lax.dynamic_slice_in_dim(valid, t0, Q_CHUNK, axis=2)
        k_sel = kb[bi, hi, ic]
        v_sel = vb[bi, hi, ic]
        sel_pos = ic[..., None] * MOBA_BLOCK + jnp.arange(MOBA_BLOCK)
        c = t0 // MOBA_BLOCK
        k_own = lax.dynamic_index_in_dim(kb, c, axis=2, keepdims=False)
        v_own = lax.dynamic_index_in_dim(vb, c, axis=2, keepdims=False)
        own_pos = c * MOBA_BLOCK + jnp.arange(MOBA_BLOCK)
        return _moba_attend(qc, t0 + jnp.arange(Q_CHUNK), k_sel, v_sel, sel_pos, vc, k_own, v_own, own_pos, slopes)

    out = lax.map(chunk, jnp.arange(T // Q_CHUNK))
    return jnp.transpose(out, (1, 0, 3, 2, 4)).reshape(B, T, H * D)


def _moba_sample(q, k, v, cache_k, cache_v, page_table, slopes):
    DB, T, H, D = q.shape
    f32 = jnp.float32
    qh, kh, vh = (jnp.transpose(t, (0, 2, 1, 3)) for t in (q, k, v))
    t_pos = PAST_LEN + jnp.arange(T)
    pages_per_block = MOBA_BLOCK // PAGE_SIZE
    n_pages = PAST_LEN // PAGE_SIZE
    nb_past = PAST_LEN // MOBA_BLOCK
    own_start = nb_past * pages_per_block
    n_own = n_pages - own_start
    own_pages = page_table[:, own_start:]
    k_own_past = jnp.transpose(cache_k[own_pages], (0, 2, 1, 3, 4)).reshape(DB, H, n_own * PAGE_SIZE, D)
    v_own_past = jnp.transpose(cache_v[own_pages], (0, 2, 1, 3, 4)).reshape(DB, H, n_own * PAGE_SIZE, D)
    k_own = jnp.concatenate([k_own_past.astype(kh.dtype), kh], axis=2)
    v_own = jnp.concatenate([v_own_past.astype(vh.dtype), vh], axis=2)
    own_pos = jnp.concatenate([nb_past * MOBA_BLOCK + jnp.arange(n_own * PAGE_SIZE), t_pos])
    n_sel = min(MOBA_TOPK, nb_past)
    if n_sel > 0:
        full_pages = page_table[:, :own_start]
        page_mean = jnp.mean(cache_k, axis=2, dtype=f32)
        kmean = page_mean[full_pages].reshape(DB, nb_past, pages_per_block, H, D).mean(axis=2)
        kmean = jnp.transpose(kmean, (0, 2, 1, 3))
        gate = jnp.einsum('bhtd,bhnd->bhtn', qh.astype(f32), kmean)
        _, idx = lax.top_k(gate, n_sel)
        blk_pages = full_pages.reshape(DB, nb_past, pages_per_block)
        phys = blk_pages[jnp.arange(DB)[:, None, None, None], idx]
        hi = jnp.arange(H)[None, :, None, None, None]
        k_sel = cache_k[phys, hi].reshape(DB, H, T, n_sel, MOBA_BLOCK, D).astype(kh.dtype)
        v_sel = cache_v[phys, hi].reshape(DB, H, T, n_sel, MOBA_BLOCK, D).astype(vh.dtype)
        sel_pos = idx[..., None] * MOBA_BLOCK + jnp.arange(MOBA_BLOCK)
        sel_valid = jnp.ones(idx.shape, dtype=bool)
    else:
        k_sel = jnp.zeros((DB, H, T, 0, MOBA_BLOCK, D), kh.dtype)
        v_sel = jnp.zeros((DB, H, T, 0, MOBA_BLOCK, D), vh.dtype)
        sel_pos = jnp.zeros((DB, H, T, 0, MOBA_BLOCK), jnp.int32)
        sel_valid = jnp.zeros((DB, H, T, 0), dtype=bool)
    out = _moba_attend(qh, t_pos, k_sel, v_sel, sel_pos, sel_valid, k_own, v_own, own_pos, slopes)
    return jnp.transpose(out, (0, 2, 1, 3)).reshape(DB, T, H * D)


def _hier_moe(u, lw):
    B, T, D = u.shape
    x = u.reshape(B * T, D)
    n = x.shape[0]
    f32 = jnp.float32
    rows = jnp.arange(n)
    lg = (x @ lw['w_router_group'] + lw['b_router_group']).astype(f32)
    pg = jax.nn.softmax(lg, axis=-1)
    g_top = jnp.argmax(lg, axis=-1)
    pg_top = pg[rows, g_top][:, None]
    le = (x @ lw['w_router_expert'] + lw['b_router_expert']).astype(f32).reshape(n, N_GROUPS, EXPERTS_PER_GROUP)
    v2, i2 = lax.top_k(le[rows, g_top], EXPERT_TOPK)
    w2 = jax.nn.softmax(v2, axis=-1) * pg_top
    eid = g_top[:, None] * EXPERTS_PER_GROUP + i2
    comb = jnp.sum(jax.nn.one_hot(eid, N_EXPERTS, dtype=f32) * w2[..., None], axis=1).astype(x.dtype)
    out = jnp.zeros_like(x)
    for e in range(N_EXPERTS):
        gate, up = jnp.split(x @ lw['w_expert_in'][e], 2, axis=-1)
        out = out + comb[:, e:e + 1] * ((jax.nn.silu(gate) * up) @ lw['w_expert_out'][e])
    return out.reshape(B, T, D)


def _layer(h, pe, shift_prev, wkv_prev, attn, lw):
    B, T, _ = h.shape
    u = _rmsnorm(h, lw['norm_mix'])
    z = u @ lw['w_in']
    o0 = RWKV_PROJ
    o1 = o0 + ATTN_WIDTH
    o2 = o1 + ATTN_WIDTH
    o3 = o2 + ATTN_WIDTH
    o4 = o3 + D_MODEL
    z_rwkv, z_q, z_k, z_v, z_ga, z_gb = jnp.split(z, [o0, o1, o2, o3, o4], axis=-1)
    o_a, wkv_new, shift_new = _rwkv7(z_rwkv, shift_prev, wkv_prev, lw)
    to_heads = lambda t: t.reshape(B, T, ATTN_HEADS, ATTN_HEAD_DIM)
    q, k, v = to_heads(z_q), to_heads(z_k), to_heads(z_v)
    o_b = attn(q, k, v)
    mix = jax.nn.sigmoid(z_ga) * (o_a @ lw['w_branch_a']) + jax.nn.sigmoid(z_gb) * (o_b @ lw['w_branch_b'])
    h = h + mix @ lw['w_out']
    h = h + _hier_moe(_rmsnorm(h, lw['norm_ffn']), lw)
    u = _rmsnorm(h, lw['norm_ple'])
    h = h + jax.nn.sigmoid(u @ lw['w_ple_gate'] + lw['b_ple_gate']) * (pe @ lw['w_ple_proj'])
    return h, k, v, wkv_new, shift_new


def setup_inputs(seed: int = 0) -> dict:
    key = jax.random.key(seed)
    ks = iter(jax.random.split(key, 48))
    f32 = jnp.float32
    nrm = lambda shape, scale=1.0: jax.random.normal(next(ks), shape, f32) * scale
    L, D = DEPTH, D_MODEL
    n_pages = PAST_LEN // PAGE_SIZE
    n_used = DEC_BATCH * n_pages
    n_pool = n_used + n_used // 4
    page_table = jax.random.permutation(next(ks), n_pool)[:n_used].reshape(DEC_BATCH, n_pages).astype(jnp.int32)
    return {
        'x_prompt': nrm((BATCH, SEQ, D)),
        'x_sample': nrm((DEC_BATCH, DEC_SEQ, D)),
        'p_prompt': nrm((DEPTH, BATCH, SEQ, PLE_DIM)),
        'p_sample': nrm((DEPTH, DEC_BATCH, DEC_SEQ, PLE_DIM)),
        'cache_k': nrm((DEPTH, n_pool, ATTN_HEADS, PAGE_SIZE, ATTN_HEAD_DIM)),
        'cache_v': nrm((DEPTH, n_pool, ATTN_HEADS, PAGE_SIZE, ATTN_HEAD_DIM)),
        'page_table': page_table,
        'state_wkv': nrm((DEPTH, DEC_BATCH, RWKV_HEADS, RWKV_HEAD_DIM, RWKV_HEAD_DIM), 0.5),
        'state_shift': nrm((DEPTH, DEC_BATCH, RWKV_PROJ)),
        'norm_mix': 1.0 + nrm((L, D), 0.02),
        'w_in': nrm((L, D, IN_PROJ), D ** -0.5),
        'shift_mu': jax.random.uniform(next(ks), (L, RWKV_PROJ), f32),
        'decay_w0': jax.random.uniform(next(ks), (L, RWKV_WIDTH), f32, -6.0, -1.0),
        'decay_w2': nrm((L, DECAY_LORA, RWKV_WIDTH), 0.1 * DECAY_LORA ** -0.5),
        'iclr_a0': nrm((L, RWKV_WIDTH), 0.1),
        'iclr_a2': nrm((L, AAA_LORA, RWKV_WIDTH), 0.1 * AAA_LORA ** -0.5),
        'gate_g2': nrm((L, GATE_LORA, RWKV_WIDTH), GATE_LORA ** -0.5),
        'k_k': 0.85 + nrm((L, RWKV_WIDTH), 0.02),
        'k_a': 1.0 + nrm((L, RWKV_WIDTH), 0.02),
        'r_k': nrm((L, RWKV_HEADS, RWKV_HEAD_DIM), 0.1),
        'ln_x_w': 1.0 + nrm((L, RWKV_WIDTH), 0.02),
        'ln_x_b': nrm((L, RWKV_WIDTH), 0.02),
        'w_branch_a': nrm((L, RWKV_WIDTH, D), RWKV_WIDTH ** -0.5),
        'w_branch_b': nrm((L, ATTN_WIDTH, D), ATTN_WIDTH ** -0.5),
        'w_out': nrm((L, D, D), D ** -0.5),
        'norm_ffn': 1.0 + nrm((L, D), 0.02),
        'w_router_group': nrm((L, D, N_GROUPS), D ** -0.5),
        'b_router_group': nrm((L, N_GROUPS), 0.01),
        'w_router_expert': nrm((L, D, N_EXPERTS), D ** -0.5),
        'b_router_expert': nrm((L, N_EXPERTS), 0.01),
        'w_expert_in': nrm((L, N_EXPERTS, D, 2 * D_EXPERT), D ** -0.5),
        'w_expert_out': nrm((L, N_EXPERTS, D_EXPERT, D), D_EXPERT ** -0.5),
        'norm_ple': 1.0 + nrm((L, D), 0.02),
        'w_ple_gate': nrm((L, D, D), D ** -0.5),
        'b_ple_gate': nrm((L, D), 0.02),
        'w_ple_proj': nrm((L, PLE_DIM, D), PLE_DIM ** -0.5),
        'norm_final': 1.0 + nrm((D,), 0.02),
    }


def reference(x_prompt, x_sample, p_prompt, p_sample, cache_k, cache_v, page_table, state_wkv, state_shift,
              norm_mix, w_in, shift_mu, decay_w0, decay_w2, iclr_a0, iclr_a2, gate_g2, k_k, k_a, r_k,
              ln_x_w, ln_x_b, w_branch_a, w_branch_b, w_out, norm_ffn, w_router_group, b_router_group,
              w_router_expert, b_router_expert, w_expert_in, w_expert_out, norm_ple, w_ple_gate, b_ple_gate,
              w_ple_proj, norm_final):
    slopes = _alibi_slopes(ATTN_HEADS)
    h_p, h_s = x_prompt, x_sample
    B, T = x_prompt.shape[0], x_prompt.shape[1]
    kp_l, vp_l, ks_l, vs_l, wp_l, ws_l, sp_l, ss_l = [], [], [], [], [], [], [], []
    for i in range(DEPTH):
        lw = dict(norm_mix=norm_mix[i], w_in=w_in[i], shift_mu=shift_mu[i], decay_w0=decay_w0[i],
                  decay_w2=decay_w2[i], iclr_a0=iclr_a0[i], iclr_a2=iclr_a2[i], gate_g2=gate_g2[i],
                  k_k=k_k[i], k_a=k_a[i], r_k=r_k[i], ln_x_w=ln_x_w[i], ln_x_b=ln_x_b[i],
                  w_branch_a=w_branch_a[i], w_branch_b=w_branch_b[i], w_out=w_out[i], norm_ffn=norm_ffn[i],
                  w_router_group=w_router_group[i], b_router_group=b_router_group[i],
                  w_router_expert=w_router_expert[i], b_router_expert=b_router_expert[i],
                  w_expert_in=w_expert_in[i], w_expert_out=w_expert_out[i], norm_ple=norm_ple[i],
                  w_ple_gate=w_ple_gate[i], b_ple_gate=b_ple_gate[i], w_ple_proj=w_ple_proj[i])
        attn_p = functools.partial(_moba_prompt, slopes=slopes)
        zero_shift = jnp.zeros((B, RWKV_PROJ), h_p.dtype)
        zero_wkv = jnp.zeros((B, RWKV_HEADS, RWKV_HEAD_DIM, RWKV_HEAD_DIM), h_p.dtype)
        h_p, k_p, v_p, wkv_p, sh_p = _layer(h_p, p_prompt[i], zero_shift, zero_wkv, attn_p, lw)
        to_pages = lambda t: jnp.transpose(t.reshape(B, T // PAGE_SIZE, PAGE_SIZE, ATTN_HEADS, ATTN_HEAD_DIM), (0, 1, 3, 2, 4))
        kp_l.append(to_pages(k_p))
        vp_l.append(to_pages(v_p))
        wp_l.append(wkv_p)
        sp_l.append(sh_p)
        attn_s = functools.partial(_moba_sample, cache_k=cache_k[i], cache_v=cache_v[i], page_table=page_table, slopes=slopes)
        h_s, k_s, v_s, wkv_s, sh_s = _layer(h_s, p_sample[i], state_shift[i], state_wkv[i], attn_s, lw)
        ks_l.append(jnp.transpose(k_s, (0, 2, 1, 3)))
        vs_l.append(jnp.transpose(v_s, (0, 2, 1, 3)))
        ws_l.append(wkv_s)
        ss_l.append(sh_s)
    y_prompt = _rmsnorm(h_p, norm_final)
    y_sample = _rmsnorm(h_s, norm_final)
    k_prompt = jnp.stack(kp_l)
    v_prompt = jnp.stack(vp_l)
    k_sample = jnp.stack(ks_l)
    v_sample = jnp.stack(vs_l)
    wkv_prompt = jnp.stack(wp_l)
    wkv_sample = jnp.stack(ws_l)
    shift_prompt = jnp.stack(sp_l)
    shift_sample = jnp.stack(ss_l)
    return (y_prompt, y_sample, k_prompt, v_prompt, k_sample, v_sample, wkv_prompt, wkv_sample, shift_prompt, shift_sample)
```

```python
import functools

import jax
import jax.numpy as jnp
import numpy as np
from jax import lax
from jax.experimental import pallas as pl
from jax.experimental.pallas import tpu as pltpu

F32 = jnp.float32
BF16 = jnp.bfloat16
HIGHEST = lax.Precision.HIGHEST

D_MODEL = 2048
BATCH = 2
SEQ = 4096
DEC_BATCH = 32
DEC_SEQ = 4
PAST_LEN = 16384
PAGE_SIZE = 128
RWKV_HEADS = 16
RWKV_HEAD_DIM = 64
RWKV_WIDTH = 1024
DECAY_LORA = 64
AAA_LORA = 64
GATE_LORA = 160
LORA_W = DECAY_LORA + AAA_LORA + GATE_LORA
RWKV_PROJ = 3 * RWKV_WIDTH + LORA_W
GN_EPS = 64e-5
ATTN_HEADS = 8
ATTN_HEAD_DIM = 128
ATTN_WIDTH = 1024
MOBA_BLOCK = 256
MOBA_TOPK = 3
IN_PROJ = RWKV_PROJ + 3 * ATTN_WIDTH + 2 * D_MODEL
N_GROUPS = 4
EXPERTS_PER_GROUP = 8
N_EXPERTS = 32
D_EXPERT = 256
PLE_DIM = 256
NORM_EPS = 1e-6

N_PROMPT = BATCH * SEQ
N_SAMPLE = DEC_BATCH * DEC_SEQ
N_TOK = N_PROMPT + N_SAMPLE
N_BLOCKS = SEQ // MOBA_BLOCK
N_PAGES = PAST_LEN // PAGE_SIZE
N_PAST_BLOCKS = PAST_LEN // MOBA_BLOCK

LORA_PAD = 512
ZP_W = 3 * RWKV_WIDTH + 3 * ATTN_WIDTH + 2 * D_MODEL + LORA_PAD
C_Q, C_K, C_V = 3072, 4096, 5120
C_GA, C_GB, C_LORA = 6144, 8192, 10240

TM = 640
TN = 512
VMEM_LIMIT = 56 * 1024 * 1024
NEG = -1e30


def _cp(sem, vmem=VMEM_LIMIT):
    return pltpu.CompilerParams(dimension_semantics=sem, vmem_limit_bytes=vmem)


def _hdot(a, b):
    return jnp.dot(a, b, precision=HIGHEST, preferred_element_type=F32)


def _bdot(a, b):
    return jnp.dot(a.astype(BF16), b.astype(BF16), preferred_element_type=F32)


def _rms(x, g):
    return x * lax.rsqrt(jnp.mean(x * x, axis=-1, keepdims=True) + NORM_EPS) * g


def _rms_mm_kernel(x_ref, g_ref, w_ref, o_ref, u_ref):
    @pl.when(pl.program_id(1) == 0)
    def _():
        u_ref[...] = _rms(x_ref[...], g_ref[...]).astype(BF16)

    o_ref[...] = jnp.dot(u_ref[...], w_ref[...].astype(BF16), preferred_element_type=F32)


def _rms_matmul(x, g, w):
    n, k = x.shape
    nout = w.shape[1]
    return pl.pallas_call(
        _rms_mm_kernel,
        out_shape=jax.ShapeDtypeStruct((n, nout), F32),
        grid=(n // TM, nout // TN),
        in_specs=[
            pl.BlockSpec((TM, k), lambda i, j: (i, 0)),
            pl.BlockSpec((1, k), lambda i, j: (0, 0)),
            pl.BlockSpec((k, TN), lambda i, j: (0, j)),
        ],
        out_specs=pl.BlockSpec((TM, TN), lambda i, j: (i, j)),
        scratch_shapes=[pltpu.VMEM((TM, k), BF16)],
        compiler_params=_cp(("parallel", "arbitrary")),
        name="rms_in_proj",
    )(x, g, w)


def _rwkv_prep_math(zr, zk, zv, zl, pr, pk, pv, plo, prm, outs):
    (mu_r, mu_k, mu_v, mu_l, w0, a0, kk_w, ka_w, rk_w, d2, a2, g2, seg, segt) = prm
    r_o, w_o, km_o, v_o, nkk_o, b_o, g_o, bv_o = outs
    zs_r = zr + (pr - zr) * mu_r[...]
    zs_k = zk + (pk - zk) * mu_k[...]
    zs_v = zv + (pv - zv) * mu_v[...]
    zs_l = zl + (plo - zl) * mu_l[...]
    xw = w0[...] + _bdot(jnp.tanh(zs_l), d2[...])
    w_log = -jax.nn.softplus(-xw) - 0.5
    decay = jnp.exp(-jnp.exp(w_log))
    a = jax.nn.sigmoid(a0[...] + _bdot(zs_l, a2[...]))
    g = _bdot(jax.nn.sigmoid(zs_l), g2[...])
    kkraw = zs_k * kk_w[...]
    ss = _hdot(kkraw * kkraw, seg[...])
    inv = lax.rsqrt(jnp.maximum(ss, 1e-24))
    kk = kkraw * _hdot(inv, segt[...])
    k_mod = zs_k * (1.0 + (a - 1.0) * ka_w[...])
    bonus = _hdot(zs_r * k_mod * rk_w[...], seg[...])
    r_o[...] = zs_r
    w_o[...] = decay
    km_o[...] = k_mod
    v_o[...] = zs_v
    nkk_o[...] = -kk
    b_o[...] = kk * a
    g_o[...] = g
    bv_o[...] = _hdot(bonus, segt[...]) * zs_v


def _shift_rows(z, row0):
    rolled = pltpu.roll(z, 1, 0)
    rid = lax.broadcasted_iota(jnp.int32, z.shape, 0)
    return jnp.where(rid == 0, row0, rolled)


def _rwkv_prep_prompt_kernel(tiles_per_seq, zr, zk, zv, zl, p8r, p8k, p8v, p8l, *rest):
    prm, outs = rest[:14], rest[14:]
    first = (pl.program_id(0) % tiles_per_seq) == 0

    def prev(z_ref, p8_ref):
        row0 = jnp.where(first, 0.0, p8_ref[7:8, :])
        return _shift_rows(z_ref[...], row0)

    _rwkv_prep_math(zr[...], zk[...], zv[...], zl[...], prev(zr, p8r), prev(zk, p8k), prev(zv, p8v),
                    prev(zl, p8l), prm, outs)


def _rwkv_prep_sample_kernel(zr, zk, zv, zl, br, bk, bv, bl, *rest):
    prm, outs = rest[:14], rest[14:]

    def prev(z_ref, b_ref):
        z = z_ref[...]
        rid = lax.broadcasted_iota(jnp.int32, z.shape, 0)
        return jnp.where(rid % DEC_SEQ == 0, b_ref[...], pltpu.roll(z, 1, 0))

    _rwkv_prep_math(zr[...], zk[...], zv[...], zl[...], prev(zr, br), prev(zk, bk), prev(zv, bv),
                    prev(zl, bl), prm, outs)


def _rwkv_prep(zp, prm, state_shift):
    W = RWKV_WIDTH
    prm_specs_1 = [pl.BlockSpec(p.shape, lambda i: (0, 0)) for p in prm]
    out_names = 8

    tmp = 256
    tps = SEQ // tmp
    cb = lambda c: (lambda i: (i, c))
    p8 = lambda c, wblk: (lambda i: (jnp.maximum(i * (tmp // 8) - 1, 0), c))
    in_specs = [
        pl.BlockSpec((tmp, W), cb(0)), pl.BlockSpec((tmp, W), cb(1)), pl.BlockSpec((tmp, W), cb(2)),
        pl.BlockSpec((tmp, LORA_PAD), cb(C_LORA // LORA_PAD)),
        pl.BlockSpec((8, W), p8(0, W)), pl.BlockSpec((8, W), p8(1, W)), pl.BlockSpec((8, W), p8(2, W)),
        pl.BlockSpec((8, LORA_PAD), p8(C_LORA // LORA_PAD, LORA_PAD)),
    ] + prm_specs_1
    outs_p = pl.pallas_call(
        functools.partial(_rwkv_prep_prompt_kernel, tps),
        out_shape=[jax.ShapeDtypeStruct((N_PROMPT, W), F32)] * out_names,
        grid=(N_PROMPT // tmp,),
        in_specs=in_specs,
        out_specs=[pl.BlockSpec((tmp, W), lambda i: (i, 0))] * out_names,
        compiler_params=_cp(("parallel",)),
        name="rwkv_prep_prompt",
    )(zp, zp, zp, zp, zp, zp, zp, zp, *prm)

    bnd = jnp.repeat(state_shift, DEC_SEQ, axis=0)
    bnd_l = jnp.pad(bnd[:, 3 * W:], ((0, 0), (0, LORA_PAD - LORA_W)))
    rb = N_PROMPT // N_SAMPLE
    cs = lambda c: (lambda i: (rb, c))
    full = lambda shape: pl.BlockSpec(shape, lambda i: (0, 0))
    in_specs = [
        pl.BlockSpec((N_SAMPLE, W), cs(0)), pl.BlockSpec((N_SAMPLE, W), cs(1)), pl.BlockSpec((N_SAMPLE, W), cs(2)),
        pl.BlockSpec((N_SAMPLE, LORA_PAD), cs(C_LORA // LORA_PAD)),
        full((N_SAMPLE, W)), full((N_SAMPLE, W)), full((N_SAMPLE, W)), full((N_SAMPLE, LORA_PAD)),
    ] + prm_specs_1
    outs_s = pl.pallas_call(
        _rwkv_prep_sample_kernel,
        out_shape=[jax.ShapeDtypeStruct((N_SAMPLE, W), F32)] * out_names,
        grid=(1,),
        in_specs=in_specs,
        out_specs=[pl.BlockSpec((N_SAMPLE, W), lambda i: (0, 0))] * out_names,
        compiler_params=_cp(("arbitrary",)),
        name="rwkv_prep_sample",
    )(zp, zp, zp, zp, bnd[:, :W], bnd[:, W:2 * W], bnd[:, 2 * W:3 * W], bnd_l, *prm)
    return outs_p, outs_s


def _wkv_scan_kernel(nkk_ref, w_ref, b_ref, k_ref, r_ref, v_ref, s0_ref, y_ref, sf_ref, st_ref):
    tc = pl.program_id(1)

    @pl.when(tc == 0)
    def _():
        st_ref[...] = s0_ref[...]

    n_steps = y_ref.shape[0]

    def step(t, carry):
        nkk = nkk_ref[t]
        w = w_ref[t]
        bb = b_ref[t]
        kk = k_ref[t]
        r = r_ref[t]
        vrow = v_ref[t]
        ys = []
        for g in range(16):
            s = st_ref[g]
            sa = jnp.sum(s * nkk, axis=0, keepdims=True)
            sn = s * w + sa * bb + vrow[g:g + 1, :] * kk
            st_ref[g] = sn
            ys.append(jnp.sum(sn * r, axis=0, keepdims=True))
        y_ref[t] = jnp.concatenate(ys, axis=0)
        return carry

    lax.fori_loop(0, n_steps, step, 0)

    @pl.when(tc == pl.num_programs(1) - 1)
    def _():
        sf_ref[...] = st_ref[...]


def _wkv_scan(cols, vrow, s0, chunk):
    g_, t_ = vrow.shape[0], vrow.shape[1]
    col_spec = pl.BlockSpec((None, chunk, 64, 128), lambda g, t: (g, t, 0, 0))
    row_spec = pl.BlockSpec((None, chunk, 16, 128), lambda g, t: (g, t, 0, 0))
    st_spec = pl.BlockSpec((None, 16, 64, 128), lambda g, t: (g, 0, 0, 0))
    return pl.pallas_call(
        _wkv_scan_kernel,
        out_shape=[jax.ShapeDtypeStruct((g_, t_, 16, 128), F32), jax.ShapeDtypeStruct((g_, 16, 64, 128), F32)],
        grid=(g_, t_ // chunk),
        in_specs=[col_spec] * 5 + [row_spec, st_spec],
        out_specs=[row_spec, st_spec],
        scratch_shapes=[pltpu.VMEM((16, 64, 128), F32)],
        compiler_params=_cp(("parallel", "arbitrary")),
        name="wkv_scan",
    )(*cols, vrow, s0)


def _to_cols(x, groups, t_):
    y = x.reshape(groups, 2, t_, RWKV_HEADS, RWKV_HEAD_DIM).transpose(0, 2, 4, 1, 3).reshape(groups, t_, 64, 32)
    return jnp.repeat(y, 4, axis=-1)


def _to_rows(x, groups, t_):
    return x.reshape(groups, 2, t_, RWKV_HEADS, 16, 4).transpose(0, 2, 4, 1, 3, 5).reshape(groups, t_, 16, 128)


def _from_rows(y, groups, t_):
    return y.reshape(groups, t_, 16, 2, RWKV_HEADS, 4).transpose(0, 3, 1, 4, 2, 5).reshape(groups * 2 * t_, RWKV_WIDTH)


def _state_to_tiles(s, groups):
    return s.reshape(groups, 2, RWKV_HEADS, 16, 4, 64).transpose(0, 3, 5, 1, 2, 4).reshape(groups, 16, 64, 128)


def _state_from_tiles(s, groups):
    return s.reshape(groups, 16, 64, 2, RWKV_HEADS, 4).transpose(0, 3, 4, 1, 5, 2).reshape(
        groups * 2, RWKV_HEADS, 64, 64)


def _rwkv_post_kernel(y_ref, bv_ref, g_ref, lw_ref, lb_ref, seg_ref, segt_ref, o_ref):
    y = y_ref[...]
    mean = _hdot(_hdot(y, seg_ref[...]) * (1.0 / RWKV_HEAD_DIM), segt_ref[...])
    yc = y - mean
    var = _hdot(yc * yc, seg_ref[...]) * (1.0 / RWKV_HEAD_DIM)
    rstd = _hdot(lax.rsqrt(var + GN_EPS), segt_ref[...])
    yn = yc * rstd * lw_ref[...] + lb_ref[...]
    o_ref[...] = ((yn + bv_ref[...]) * g_ref[...]).astype(BF16)


def _rwkv_post(y, bv, g, lw, lb, seg, segt):
    n = y.shape[0]
    tm = 128 if n == N_SAMPLE else 512
    row = pl.BlockSpec((tm, RWKV_WIDTH), lambda i: (i, 0))
    full = lambda a: pl.BlockSpec(a.shape, lambda i: (0, 0))
    return pl.pallas_call(
        _rwkv_post_kernel,
        out_shape=jax.ShapeDtypeStruct((n, RWKV_WIDTH), BF16),
        grid=(n // tm,),
        in_specs=[row, row, row, full(lw), full(lb), full(seg), full(segt)],
        out_specs=row,
        compiler_params=_cp(("parallel",)),
        name="rwkv_post",
    )(y, bv, g, lw, lb, seg, segt)


def _kmean_prompt_kernel(k_ref, o_ref):
    k = k_ref[...]
    o_ref[...] = jnp.mean(k.reshape(8, MOBA_BLOCK, ATTN_WIDTH), axis=1)


def _kmean_prompt(zp):
    nblk = N_PROMPT // MOBA_BLOCK
    return pl.pallas_call(
        _kmean_prompt_kernel,
        out_shape=jax.ShapeDtypeStruct((nblk, ATTN_WIDTH), F32),
        grid=(nblk // 8,),
        in_specs=[pl.BlockSpec((8 * MOBA_BLOCK, ATTN_WIDTH), lambda i: (i, C_K // ATTN_WIDTH))],
        out_specs=pl.BlockSpec((8, ATTN_WIDTH), lambda i: (i, 0)),
        compiler_params=_cp(("parallel",)),
        name="moba_kmean_prompt",
    )(zp)


def _moba_prompt_kernel(slopes_ref, q_ref, k_ref, v_ref, km_ref, o_ref, sel_ref):
    h = pl.program_id(1)
    c = pl.program_id(2)
    slope = slopes_ref[h]
    scale = ATTN_HEAD_DIM ** -0.5
    bs = MOBA_BLOCK
    q = q_ref[...]
    qb = q.astype(BF16)

    gate = lax.dot_general(km_ref[...], q, (((1,), (1,)), ((), ())), precision=HIGHEST,
                           preferred_element_type=F32)
    blk = lax.broadcasted_iota(jnp.int32, gate.shape, 0)
    valid = blk < c
    gate = jnp.where(valid, gate, -jnp.inf)
    rank = jnp.zeros(gate.shape, jnp.int32)
    for m in range(N_BLOCKS):
        gm = gate[m:m + 1, :]
        ahead = (gm > gate) | ((gm == gate) & (m < blk))
        rank = rank + ahead.astype(jnp.int32)
    sel_ref[...] = jnp.where(valid & (rank < MOBA_TOPK), 1.0, 0.0)

    kj = lax.broadcasted_iota(jnp.int32, (bs, bs), 0)
    qi = lax.broadcasted_iota(jnp.int32, (bs, bs), 1)
    rel = (qi - kj).astype(F32)

    def scores(n):
        off = pl.multiple_of(n * bs, bs)
        kb = k_ref[pl.ds(off, bs), :].astype(BF16)
        st = lax.dot_general(kb, qb, (((1,), (1,)), ((), ())), preferred_element_type=F32)
        dist = rel + ((c - n) * bs).astype(F32)
        return st * scale - slope * dist, off

    def pv(p, off):
        vt = v_ref[pl.ds(off, bs), :].T.astype(BF16)
        return jnp.dot(vt, p.astype(BF16), preferred_element_type=F32)

    st, off = scores(c)
    st = jnp.where(kj <= qi, st, -jnp.inf)
    m0 = jnp.max(st, axis=0, keepdims=True)
    p = jnp.exp(st - m0)
    l0 = jnp.sum(p, axis=0, keepdims=True)
    acc0 = pv(p, off)

    def body(n, carry):
        m, l, acc = carry
        st, off = scores(n)
        st = jnp.where(sel_ref[pl.ds(n, 1), :] > 0.5, st, -jnp.inf)
        m_new = jnp.maximum(m, jnp.max(st, axis=0, keepdims=True))
        alpha = jnp.exp(m - m_new)
        p = jnp.exp(st - m_new)
        l = alpha * l + jnp.sum(p, axis=0, keepdims=True)
        acc = alpha * acc + pv(p, off)
        return m_new, l, acc

    m, l, acc = lax.fori_loop(0, c, body, (m0, l0, acc0))
    o_ref[...] = (acc / l).T.astype(BF16)


def _moba_prompt(zp, kmean, slopes):
    bs = MOBA_BLOCK
    hd = ATTN_HEAD_DIM
    grid_spec = pltpu.PrefetchScalarGridSpec(
        num_scalar_prefetch=1,
        grid=(BATCH, ATTN_HEADS, N_BLOCKS),
        in_specs=[
            pl.BlockSpec((bs, hd), lambda b, h, c, s: (b * N_BLOCKS + c, C_Q // hd + h)),
            pl.BlockSpec((SEQ, hd), lambda b, h, c, s: (b, C_K // hd + h)),
            pl.BlockSpec((SEQ, hd), lambda b, h, c, s: (b, C_V // hd + h)),
            pl.BlockSpec((N_BLOCKS, hd), lambda b, h, c, s: (b, h)),
        ],
        out_specs=pl.BlockSpec((bs, hd), lambda b, h, c, s: (b * N_BLOCKS + c, h)),
        scratch_shapes=[pltpu.VMEM((N_BLOCKS, bs), F32)],
    )
    return pl.pallas_call(
        _moba_prompt_kernel,
        out_shape=jax.ShapeDtypeStruct((N_PROMPT, ATTN_WIDTH), BF16),
        grid_spec=grid_spec,
        compiler_params=_cp(("parallel", "parallel", "arbitrary")),
        name="moba_prompt_attn",
    )(slopes, zp, zp, zp, kmean)


def _page_mean_kernel(pt_ref, k_ref, o_ref):
    o_ref[...] = jnp.mean(k_ref[...], axis=1)


def _page_mean(cache_k, page_table):
    grid_spec = pltpu.PrefetchScalarGridSpec(
        num_scalar_prefetch=1,
        grid=(DEC_BATCH, N_PAGES),
        in_specs=[pl.BlockSpec((None, ATTN_HEADS, PAGE_SIZE, ATTN_HEAD_DIM),
                               lambda b, j, pt: (pt[b, j], 0, 0, 0))],
        out_specs=pl.BlockSpec((None, None, ATTN_HEADS, ATTN_HEAD_DIM), lambda b, j, pt: (b, j, 0, 0)),
    )
    return pl.pallas_call(
        _page_mean_kernel,
        out_shape=jax.ShapeDtypeStruct((DEC_BATCH, N_PAGES, ATTN_HEADS, ATTN_HEAD_DIM), F32),
        grid_spec=grid_spec,
        compiler_params=_cp(("parallel", "arbitrary")),
        name="moba_page_mean",
    )(page_table, cache_k)


def _sample_gate_kernel(q_ref, pm_ref, pair_ref, o_ref):
    km = _hdot(pair_ref[...], pm_ref[...])
    gate = lax.dot_general(q_ref[...], km, (((1,), (1,)), ((), ())), precision=HIGHEST,
                           preferred_element_type=F32)
    lane = lax.broadcasted_iota(jnp.int32, gate.shape, 1)
    out_lane = lax.broadcasted_iota(jnp.int32, (8, 128), 1)
    out = jnp.zeros((8, 128), jnp.int32)
    for s in range(MOBA_TOPK):
        mx = jnp.max(gate, axis=1, keepdims=True)
        idx = jnp.min(jnp.where(gate == mx, lane, N_PAST_BLOCKS), axis=1, keepdims=True)
        out = jnp.where(out_lane == s, idx, out)
        gate = jnp.where(lane == idx, -jnp.inf, gate)
    o_ref[...] = out


def _sample_gate(q8, pm_t, pair):
    return pl.pallas_call(
        _sample_gate_kernel,
        out_shape=jax.ShapeDtypeStruct((DEC_BATCH, ATTN_HEADS, 8, 128), jnp.int32),
        grid=(DEC_BATCH, ATTN_HEADS),
        in_specs=[
            pl.BlockSpec((None, None, 8, ATTN_HEAD_DIM), lambda b, h: (b, h, 0, 0)),
            pl.BlockSpec((None, None, N_PAGES, ATTN_HEAD_DIM), lambda b, h: (b, h, 0, 0)),
            pl.BlockSpec(pair.shape, lambda b, h: (0, 0)),
        ],
        out_specs=pl.BlockSpec((None, None, 8, 128), lambda b, h: (b, h, 0, 0)),
        compiler_params=_cp(("parallel", "parallel")),
        name="moba_sample_gate",
    )(q8, pm_t, pair)


_PAGES_PER_QUERY = MOBA_TOPK * (MOBA_BLOCK // PAGE_SIZE)
_SAMPLE_STEPS = DEC_SEQ * _PAGES_PER_QUERY


def _sel_block(idx_ref, b, h, j):
    t = j // _PAGES_PER_QUERY
    s = (j % _PAGES_PER_QUERY) // 2
    return idx_ref[((b * ATTN_HEADS + h) * DEC_SEQ + t) * MOBA_TOPK + s]


def _sample_attn_kernel(pt_ref, idx_ref, slopes_ref, q_ref, k_ref, v_ref, ko_ref, vo_ref, o_ref,
                        m_ref, l_ref, acc_ref):
    b, h, j = pl.program_id(0), pl.program_id(1), pl.program_id(2)
    slope = slopes_ref[h]
    scale = ATTN_HEAD_DIM ** -0.5

    @pl.when(j == 0)
    def _():
        m_ref[...] = jnp.full(m_ref.shape, NEG, F32)
        l_ref[...] = jnp.zeros(l_ref.shape, F32)
        acc_ref[...] = jnp.zeros(acc_ref.shape, F32)

    t = j // _PAGES_PER_QUERY
    p_in_blk = j % 2
    blk = _sel_block(idx_ref, b, h, j)
    qb = q_ref[...].astype(BF16)
    row = lax.broadcasted_iota(jnp.int32, (8, PAGE_SIZE), 0)
    lane = lax.broadcasted_iota(jnp.int32, (8, PAGE_SIZE), 1)

    sc = lax.dot_general(qb, k_ref[...].astype(BF16), (((1,), (1,)), ((), ())), preferred_element_type=F32)
    pos = blk * MOBA_BLOCK + p_in_blk * PAGE_SIZE + lane
    dist = (PAST_LEN + t - pos).astype(F32)
    sc = sc * scale - slope * dist
    mine = lax.broadcasted_iota(jnp.int32, (8, 1), 0) == t
    m_old = m_ref[...]
    m_new = jnp.where(mine, jnp.maximum(m_old, jnp.max(sc, axis=1, keepdims=True)), m_old)
    alpha = jnp.exp(m_old - m_new)
    p = jnp.where(mine, jnp.exp(sc - m_new), 0.0)
    l_ref[...] = alpha * l_ref[...] + jnp.sum(p, axis=1, keepdims=True)
    acc_ref[...] = alpha * acc_ref[...] + jnp.dot(p.astype(BF16), v_ref[...].astype(BF16),
                                                  preferred_element_type=F32)
    m_ref[...] = m_new

    @pl.when(j == _SAMPLE_STEPS - 1)
    def _():
        so = lax.dot_general(qb, ko_ref[...].astype(BF16), (((1,), (1,)), ((), ())),
                             preferred_element_type=F32)
        so = so * scale - slope * (row - lane).astype(F32)
        ok = (lane <= row) & (lane < DEC_SEQ)
        so = jnp.where(ok, so, -jnp.inf)
        m1 = m_ref[...]
        m2 = jnp.maximum(m1, jnp.max(so, axis=1, keepdims=True))
        al = jnp.exp(m1 - m2)
        po = jnp.exp(so - m2)
        l2 = al * l_ref[...] + jnp.sum(po, axis=1, keepdims=True)
        acc = al * acc_ref[...] + jnp.dot(po.astype(BF16), vo_ref[...].astype(BF16), preferred_element_type=F32)
        o_ref[...] = acc / l2


def _sample_attn(page_table, idx_flat, slopes, q8, cache_k, cache_v, k_own, v_own):
    hd = ATTN_HEAD_DIM

    def page_map(b, h, j, pt, idx, sl):
        blk = _sel_block(idx, b, h, j)
        return (pt[b, blk * 2 + j % 2], h, 0, 0)

    qmap = lambda b, h, j, pt, idx, sl: (b, h, 0, 0)
    grid_spec = pltpu.PrefetchScalarGridSpec(
        num_scalar_prefetch=3,
        grid=(DEC_BATCH, ATTN_HEADS, _SAMPLE_STEPS),
        in_specs=[
            pl.BlockSpec((None, None, 8, hd), qmap),
            pl.BlockSpec((None, None, PAGE_SIZE, hd), page_map),
            pl.BlockSpec((None, None, PAGE_SIZE, hd), page_map),
            pl.BlockSpec((None, None, PAGE_SIZE, hd), qmap),
            pl.BlockSpec((None, None, PAGE_SIZE, hd), qmap),
        ],
        out_specs=pl.BlockSpec((None, None, 8, hd), qmap),
        scratch_shapes=[pltpu.VMEM((8, 1), F32), pltpu.VMEM((8, 1), F32), pltpu.VMEM((8, hd), F32)],
    )
    return pl.pallas_call(
        _sample_attn_kernel,
        out_shape=jax.ShapeDtypeStruct((DEC_BATCH, ATTN_HEADS, 8, hd), F32),
        grid_spec=grid_spec,
        compiler_params=_cp(("parallel", "parallel", "arbitrary")),
        name="moba_sample_attn",
    )(page_table, idx_flat, slopes, q8, cache_k, cache_v, k_own, v_own)


def _branch_mix_kernel(oa_ref, ob_ref, wa_ref, wb_ref, ga_ref, gb_ref, o_ref):
    ya = jnp.dot(oa_ref[...], wa_ref[...].astype(BF16), preferred_element_type=F32)
    yb = jnp.dot(ob_ref[...], wb_ref[...].astype(BF16), preferred_element_type=F32)
    o_ref[...] = (jax.nn.sigmoid(ga_ref[...]) * ya + jax.nn.sigmoid(gb_ref[...]) * yb).astype(BF16)


def _branch_mix(oa, ob, wa, wb, zp):
    n = oa.shape[0]
    return pl.pallas_call(
        _branch_mix_kernel,
        out_shape=jax.ShapeDtypeStruct((n, D_MODEL), BF16),
        grid=(n // TM, D_MODEL // TN),
        in_specs=[
            pl.BlockSpec((TM, RWKV_WIDTH), lambda i, j: (i, 0)),
            pl.BlockSpec((TM, ATTN_WIDTH), lambda i, j: (i, 0)),
            pl.BlockSpec((RWKV_WIDTH, TN), lambda i, j: (0, j)),
            pl.BlockSpec((ATTN_WIDTH, TN), lambda i, j: (0, j)),
            pl.BlockSpec((TM, TN), lambda i, j: (i, C_GA // TN + j)),
            pl.BlockSpec((TM, TN), lambda i, j: (i, C_GB // TN + j)),
        ],
        out_specs=pl.BlockSpec((TM, TN), lambda i, j: (i, j)),
        compiler_params=_cp(("parallel", "parallel")),
        name="branch_mix",
    )(oa, ob, wa, wb, zp, zp)


def _mm_residual_kernel(x_ref, w_ref, h_ref, o_ref):
    o_ref[...] = h_ref[...] + jnp.dot(x_ref[...], w_ref[...].astype(BF16), preferred_element_type=F32)


def _mm_residual(x, w, h):
    n, k = x.shape
    nout = w.shape[1]
    return pl.pallas_call(
        _mm_residual_kernel,
        out_shape=jax.ShapeDtypeStruct((n, nout), F32),
        grid=(n // TM, nout // TN),
        in_specs=[
            pl.BlockSpec((TM, k), lambda i, j: (i, 0)),
            pl.BlockSpec((k, TN), lambda i, j: (0, j)),
            pl.BlockSpec((TM, TN), lambda i, j: (i, j)),
        ],
        out_specs=pl.BlockSpec((TM, TN), lambda i, j: (i, j)),
        compiler_params=_cp(("parallel", "parallel")),
        name="out_proj",
    )(x, w, h)


_GROUP_LANE0 = 64


def _router_kernel(h_ref, g_ref, w_ref, b_ref, xn_ref, comb_ref):
    xn = _rms(h_ref[...], g_ref[...])
    xn_ref[...] = xn.astype(BF16)
    logits = _hdot(xn, w_ref[...]) + b_ref[...]
    lane = lax.broadcasted_iota(jnp.int32, logits.shape, 1)
    big = jnp.int32(1 << 20)
    is_g = (lane >= _GROUP_LANE0) & (lane < _GROUP_LANE0 + N_GROUPS)
    gl = jnp.where(is_g, logits, -jnp.inf)
    gmax = jnp.max(gl, axis=1, keepdims=True)
    g_top = jnp.min(jnp.where(gl == gmax, lane - _GROUP_LANE0, big), axis=1, keepdims=True)
    pg_top = 1.0 / jnp.sum(jnp.exp(gl - gmax), axis=1, keepdims=True)
    in_grp = (lane < N_EXPERTS) & ((lane // EXPERTS_PER_GROUP) == g_top)
    e1 = jnp.where(in_grp, logits, -jnp.inf)
    v1 = jnp.max(e1, axis=1, keepdims=True)
    i1 = jnp.min(jnp.where(e1 == v1, lane, big), axis=1, keepdims=True)
    e2 = jnp.where(lane == i1, -jnp.inf, e1)
    v2 = jnp.max(e2, axis=1, keepdims=True)
    i2 = jnp.min(jnp.where(e2 == v2, lane, big), axis=1, keepdims=True)
    ex = jnp.exp(v2 - v1)
    w1 = 1.0 / (1.0 + ex)
    w2 = ex / (1.0 + ex)
    comb_ref[...] = jnp.where(lane == i1, w1 * pg_top, 0.0) + jnp.where(lane == i2, w2 * pg_top, 0.0)


def _router(h, g, w, b):
    n = h.shape[0]
    return pl.pallas_call(
        _router_kernel,
        out_shape=[jax.ShapeDtypeStruct((n, D_MODEL), BF16), jax.ShapeDtypeStruct((n, 128), F32)],
        grid=(n // TM,),
        in_specs=[
            pl.BlockSpec((TM, D_MODEL), lambda i: (i, 0)),
            pl.BlockSpec((1, D_MODEL), lambda i: (0, 0)),
            pl.BlockSpec((D_MODEL, 128), lambda i: (0, 0)),
            pl.BlockSpec((1, 128), lambda i: (0, 0)),
        ],
        out_specs=[pl.BlockSpec((TM, D_MODEL), lambda i: (i, 0)), pl.BlockSpec((TM, 128), lambda i: (i, 0))],
        compiler_params=_cp(("parallel",)),
        name="moe_router",
    )(h, g, w, b)


def _moe_kernel(xn_ref, comb_ref, win_ref, wout_ref, h_ref, o_ref):
    e = pl.program_id(1)

    @pl.when(e == 0)
    def _():
        o_ref[...] = h_ref[...]

    gu = jnp.dot(xn_ref[...], win_ref[...].astype(BF16), preferred_element_type=F32)
    act = jax.nn.silu(gu[:, :D_EXPERT]) * gu[:, D_EXPERT:]
    y = jnp.dot(act.astype(BF16), wout_ref[...].astype(BF16), preferred_element_type=F32)
    comb = comb_ref[...]
    lane = lax.broadcasted_iota(jnp.int32, comb.shape, 1)
    scale = jnp.sum(jnp.where(lane == e, comb, 0.0), axis=1, keepdims=True)
    o_ref[...] += scale * y


def _moe(xn, comb, w_in, w_out, h):
    n = xn.shape[0]
    return pl.pallas_call(
        _moe_kernel,
        out_shape=jax.ShapeDtypeStruct((n, D_MODEL), F32),
        grid=(n // TM, N_EXPERTS),
        in_specs=[
            pl.BlockSpec((TM, D_MODEL), lambda i, e: (i, 0)),
            pl.BlockSpec((TM, 128), lambda i, e: (i, 0)),
            pl.BlockSpec((None, D_MODEL, 2 * D_EXPERT), lambda i, e: (e, 0, 0)),
            pl.BlockSpec((None, D_EXPERT, D_MODEL), lambda i, e: (e, 0, 0)),
            pl.BlockSpec((TM, D_MODEL), lambda i, e: (i, 0)),
        ],
        out_specs=pl.BlockSpec((TM, D_MODEL), lambda i, e: (i, 0)),
        compiler_params=_cp(("parallel", "arbitrary")),
        name="moe_experts",
    )(xn, comb, w_in, w_out, h)


def _ple_kernel(h_ref, g_ref, wg_ref, bg_ref, pe_ref, wp_ref, hres_ref, o_ref, u_ref):
    @pl.when(pl.program_id(1) == 0)
    def _():
        u_ref[...] = _rms(h_ref[...], g_ref[...]).astype(BF16)

    gate = jax.nn.sigmoid(jnp.dot(u_ref[...], wg_ref[...].astype(BF16), preferred_element_type=F32) + bg_ref[...])
    proj = jnp.dot(pe_ref[...].astype(BF16), wp_ref[...].astype(BF16), preferred_element_type=F32)
    o_ref[...] = hres_ref[...] + gate * proj


def _ple(h, g, wg, bg, pe, wp):
    n = h.shape[0]
    return pl.pallas_call(
        _ple_kernel,
        out_shape=jax.ShapeDtypeStruct((n, D_MODEL), F32),
        grid=(n // TM, D_MODEL // TN),
        in_specs=[
            pl.BlockSpec((TM, D_MODEL), lambda i, j: (i, 0)),
            pl.BlockSpec((1, D_MODEL), lambda i, j: (0, 0)),
            pl.BlockSpec((D_MODEL, TN), lambda i, j: (0, j)),
            pl.BlockSpec((1, TN), lambda i, j: (0, j)),
            pl.BlockSpec((TM, PLE_DIM), lambda i, j: (i, 0)),
            pl.BlockSpec((PLE_DIM, TN), lambda i, j: (0, j)),
            pl.BlockSpec((TM, TN), lambda i, j: (i, j)),
        ],
        out_specs=pl.BlockSpec((TM, TN), lambda i, j: (i, j)),
        scratch_shapes=[pltpu.VMEM((TM, D_MODEL), BF16)],
        compiler_params=_cp(("parallel", "arbitrary")),
        name="ple",
    )(h, g, wg, bg, pe, wp, h)


def _final_norm_kernel(h_ref, g_ref, o_ref):
    o_ref[...] = _rms(h_ref[...], g_ref[...])


def _final_norm(h, g):
    n = h.shape[0]
    return pl.pallas_call(
        _final_norm_kernel,
        out_shape=jax.ShapeDtypeStruct((n, D_MODEL), F32),
        grid=(n // TM,),
        in_specs=[pl.BlockSpec((TM, D_MODEL), lambda i: (i, 0)), pl.BlockSpec((1, D_MODEL), lambda i: (0, 0))],
        out_specs=pl.BlockSpec((TM, D_MODEL), lambda i: (i, 0)),
        compiler_params=_cp(("parallel",)),
        name="final_norm",
    )(h, g)


def kernel(x_prompt, x_sample, p_prompt, p_sample, cache_k, cache_v, page_table, state_wkv, state_shift, norm_mix, w_in, shift_mu, decay_w0, decay_w2, iclr_a0, iclr_a2, gate_g2, k_k, k_a, r_k, ln_x_w, ln_x_b, w_branch_a, w_branch_b, w_out, norm_ffn, w_router_group, b_router_group, w_router_expert, b_router_expert, w_expert_in, w_expert_out, norm_ple, w_ple_gate, b_ple_gate, w_ple_proj, norm_final):
    W = RWKV_WIDTH
    row = lambda a: a.reshape(1, -1)
    x = jnp.concatenate([x_prompt.reshape(N_PROMPT, D_MODEL), x_sample.reshape(N_SAMPLE, D_MODEL)], axis=0)
    pe = jnp.concatenate([p_prompt[0].reshape(N_PROMPT, PLE_DIM), p_sample[0].reshape(N_SAMPLE, PLE_DIM)], axis=0)

    wi = w_in[0]
    w_re = jnp.concatenate([wi[:, :3 * W], wi[:, RWKV_PROJ:], wi[:, 3 * W:RWKV_PROJ],
                            jnp.zeros((D_MODEL, LORA_PAD - LORA_W), F32)], axis=1)
    zp = _rms_matmul(x, row(norm_mix[0]), w_re)

    mu = shift_mu[0]
    pad_rows = lambda a, r0: jnp.pad(a, ((r0, LORA_PAD - r0 - a.shape[0]), (0, 0)))
    head_of = np.arange(W) // RWKV_HEAD_DIM
    seg = jnp.asarray((head_of[:, None] == np.arange(128)[None, :]).astype(np.float32))
    segt = seg.T
    prm = (row(mu[:W]), row(mu[W:2 * W]), row(mu[2 * W:3 * W]),
           row(jnp.pad(mu[3 * W:], (0, LORA_PAD - LORA_W))),
           row(decay_w0[0]), row(iclr_a0[0]), row(k_k[0]), row(k_a[0]), row(r_k[0]),
           pad_rows(decay_w2[0], 0), pad_rows(iclr_a2[0], DECAY_LORA), pad_rows(gate_g2[0], DECAY_LORA + AAA_LORA),
           seg, segt)
    outs_p, outs_s = _rwkv_prep(zp, prm, state_shift[0])

    def run_scan(outs, groups, t_, s0, chunk):
        r_, w_, km_, v_, nkk_, b_, g_, bv_ = outs
        cols = [_to_cols(a, groups, t_) for a in (nkk_, w_, b_, km_, r_)]
        y, sf = _wkv_scan(cols, _to_rows(v_, groups, t_), _state_to_tiles(s0, groups), chunk)
        o = _rwkv_post(_from_rows(y, groups, t_), bv_, g_, row(ln_x_w[0]), row(ln_x_b[0]), seg, segt)
        return o, _state_from_tiles(sf, groups)

    zero_state = jnp.zeros((BATCH, RWKV_HEADS, RWKV_HEAD_DIM, RWKV_HEAD_DIM), F32)
    oa_p, wkv_p = run_scan(outs_p, 1, SEQ, zero_state, 64)
    oa_s, wkv_s = run_scan(outs_s, DEC_BATCH // 2, DEC_SEQ, state_wkv[0], DEC_SEQ)
    o_a = jnp.concatenate([oa_p, oa_s], axis=0)

    slopes = jnp.asarray(2.0 ** (-8.0 * np.arange(1, ATTN_HEADS + 1) / ATTN_HEADS), dtype=F32)
    ob_p = _moba_prompt(zp, _kmean_prompt(zp), slopes)

    heads = lambda a: a.reshape(DEC_BATCH, DEC_SEQ, ATTN_HEADS, ATTN_HEAD_DIM).transpose(0, 2, 1, 3)
    q_s = heads(zp[N_PROMPT:, C_Q:C_Q + ATTN_WIDTH])
    k_s = heads(zp[N_PROMPT:, C_K:C_K + ATTN_WIDTH])
    v_s = heads(zp[N_PROMPT:, C_V:C_V + ATTN_WIDTH])
    pad_t = lambda a, r: jnp.pad(a, ((0, 0), (0, 0), (0, r - DEC_SEQ), (0, 0)))
    q8 = pad_t(q_s, 8)
    pm = _page_mean(cache_k[0], page_table)
    pair = jnp.asarray(0.5 * (np.arange(N_PAGES)[None, :] // 2 == np.arange(N_PAST_BLOCKS)[:, None]), dtype=F32)
    idx = _sample_gate(q8, pm.transpose(0, 2, 1, 3), pair)[:, :, :DEC_SEQ, :MOBA_TOPK]
    ob_s = _sample_attn(page_table, idx.reshape(-1), slopes, q8, cache_k[0], cache_v[0],
                        pad_t(k_s, PAGE_SIZE), pad_t(v_s, PAGE_SIZE))
    ob_s = ob_s[:, :, :DEC_SEQ].transpose(0, 2, 1, 3).reshape(N_SAMPLE, ATTN_WIDTH).astype(BF16)
    o_b = jnp.concatenate([ob_p, ob_s], axis=0)

    mix = _branch_mix(o_a, o_b, w_branch_a[0], w_branch_b[0], zp)
    h1 = _mm_residual(mix, w_out[0], x)

    w_r = jnp.zeros((D_MODEL, 128), F32)
    w_r = w_r.at[:, :N_EXPERTS].set(w_router_expert[0]).at[:, _GROUP_LANE0:_GROUP_LANE0 + N_GROUPS].set(w_router_group[0])
    b_r = jnp.zeros((128,), F32)
    b_r = b_r.at[:N_EXPERTS].set(b_router_expert[0]).at[_GROUP_LANE0:_GROUP_LANE0 + N_GROUPS].set(b_router_group[0])
    xn, comb = _router(h1, row(norm_ffn[0]), w_r, row(b_r))
    h2 = _moe(xn, comb, w_expert_in[0], w_expert_out[0], h1)

    h3 = _ple(h2, row(norm_ple[0]), w_ple_gate[0], row(b_ple_gate[0]), pe, w_ple_proj[0])
    y = _final_norm(h3, row(norm_final))

    y_prompt = y[:N_PROMPT].reshape(BATCH, SEQ, D_MODEL)
    y_sample = y[N_PROMPT:].reshape(DEC_BATCH, DEC_SEQ, D_MODEL)
    to_pages = lambda a: a.reshape(BATCH, SEQ // PAGE_SIZE, PAGE_SIZE, ATTN_HEADS, ATTN_HEAD_DIM).transpose(0, 1, 3, 2, 4)[None]
    k_prompt = to_pages(zp[:N_PROMPT, C_K:C_K + ATTN_WIDTH])
    v_prompt = to_pages(zp[:N_PROMPT, C_V:C_V + ATTN_WIDTH])
    z_rwkv = lambda rows: jnp.concatenate([zp[rows, :3 * W], zp[rows, C_LORA:C_LORA + LORA_W]], axis=-1)[None]
    shift_prompt = z_rwkv(jnp.arange(1, BATCH + 1) * SEQ - 1)
    shift_sample = z_rwkv(N_PROMPT + jnp.arange(1, DEC_BATCH + 1) * DEC_SEQ - 1)
    return (y_prompt, y_sample, k_prompt, v_prompt, k_s[None], v_s[None], wkv_p[None], wkv_s[None],
            shift_prompt, shift_sample)
```
